```python
import math
import jax, jax.numpy as jnp
from jax import lax
import numpy as np

D_MODEL = 1024
BATCH = 8
SEQ = 4096
DEPTH = 4

ATT_HEADS = 8
HEAD_DIM = 64
ATT_WIDTH = ATT_HEADS * HEAD_DIM
MOBA_BLOCK = 256
MOBA_TOPK = 3
Q_CHUNK = 128
POOL_WINDOWS = (2, 4, 8, 16)
POOL_GROUPS = 4
POOL_WIDTH = D_MODEL - ATT_WIDTH
POOL_GROUP_DIM = POOL_WIDTH // POOL_GROUPS
EVEN_IN = 3 * ATT_WIDTH + POOL_WIDTH
CONV_WIDTH = 3
ODD_IN = 3 * D_MODEL
D_FF = 2816
N_EVEN = (DEPTH + 1) // 2
N_ODD = DEPTH // 2
DN_ALPHA = (2 * DEPTH) ** 0.25
DN_BETA = (8 * DEPTH) ** -0.25
LN_EPS = 1e-5
NEG = -1e30

kernel_name = "moba_pool_shortconv_deepnorm_hybrid"


def layer_norm(x, g, b):
    xf = x.astype(jnp.float32)
    mu = jnp.mean(xf, axis=-1, keepdims=True)
    var = jnp.mean(jnp.square(xf - mu), axis=-1, keepdims=True)
    return ((xf - mu) * lax.rsqrt(var + LN_EPS)).astype(x.dtype) * g + b


def causal_dwconv(u, w, b):
    K, C = w.shape
    y = lax.conv_general_dilated(
        u, w[:, None, :].astype(u.dtype), window_strides=(1,), padding=[(K - 1, 0)],
        dimension_numbers=('NWC', 'WIO', 'NWC'), feature_group_count=C)
    return y + b


def moba_attention(q, k, v):
    Bn, S, H, Dh = q.shape
    nb = -(-S // MOBA_BLOCK)
    s_pad = nb * MOBA_BLOCK
    k_top = min(MOBA_TOPK, nb)
    qh = q.transpose(0, 2, 1, 3)
    pad = ((0, 0), (0, 0), (0, s_pad - S), (0, 0))
    kb = jnp.pad(k.transpose(0, 2, 1, 3), pad).reshape(Bn, H, nb, MOBA_BLOCK, Dh)
    vb = jnp.pad(v.transpose(0, 2, 1, 3), pad).reshape(Bn, H, nb, MOBA_BLOCK, Dh)
    kmean = jnp.mean(kb.astype(jnp.float32), axis=3)
    scale = Dh ** -0.5
    b_ix = jnp.arange(Bn)[:, None, None]
    h_ix = jnp.arange(H)[None, :, None]
    blk_ids = jnp.arange(nb)
    in_blk = jnp.arange(MOBA_BLOCK)

    def chunk(c):
        q0 = c * Q_CHUNK
        qc = lax.dynamic_slice_in_dim(qh, q0, Q_CHUNK, axis=2).astype(jnp.float32) * scale
        qpos = q0 + jnp.arange(Q_CHUNK)
        blk = q0 // MOBA_BLOCK
        gate = jnp.einsum('bhqd,bhnd->bhqn', qc, kmean)
        gate = jnp.where(blk_ids < blk, gate, NEG)
        _, sel = lax.top_k(gate, k_top)
        k_own = lax.dynamic_index_in_dim(kb, blk, axis=2, keepdims=False)
        v_own = lax.dynamic_index_in_dim(vb, blk, axis=2, keepdims=False)
        kpos = blk * MOBA_BLOCK + in_blk
        s_own = jnp.einsum('bhqd,bhkd->bhqk', qc, k_own.astype(jnp.float32))
        scores = [jnp.where(kpos[None, :] <= qpos[:, None], s_own, NEG)]
        for j in range(k_top):
            k_j = kb[b_ix, h_ix, sel[..., j]]
            s_j = jnp.einsum('bhqd,bhqkd->bhqk', qc, k_j.astype(jnp.float32))
            scores.append(jnp.where(j < blk, s_j, NEG))
        p = jax.nn.softmax(jnp.concatenate(scores, axis=-1), axis=-1)
        p = p.reshape(Bn, H, Q_CHUNK, 1 + k_top, MOBA_BLOCK)
        out = jnp.einsum('bhqk,bhkd->bhqd', p[..., 0, :], v_own.astype(jnp.float32))
        for j in range(k_top):
            v_j = vb[b_ix, h_ix, sel[..., j]]
            out = out + jnp.einsum('bhqk,bhqkd->bhqd', p[..., j + 1, :], v_j.astype(jnp.float32))
        return out.astype(v.dtype)

    o = lax.map(chunk, jnp.arange(S // Q_CHUNK))
    return o.transpose(1, 0, 3, 2, 4).reshape(Bn, S, H * Dh)


def multiscale_pool(u, w_pool, b_pool, pool_scale):
    Bn, S, _ = u.shape
    uf = u.astype(jnp.float32).reshape(Bn, S, POOL_GROUPS, POOL_GROUP_DIM)
    cs = jnp.concatenate([jnp.zeros_like(uf[:, :1]), jnp.cumsum(uf, axis=1)], axis=1)
    win = jnp.array(POOL_WINDOWS, dtype=jnp.int32)
    t = jnp.arange(S, dtype=jnp.int32)[:, None]
    g = jnp.arange(POOL_GROUPS, dtype=jnp.int32)[None, :]
    lo = jnp.maximum(t + 1 - win[None, :], 0)
    window_sum = cs[:, t + 1, g] - cs[:, lo, g]
    count = jnp.minimum(t + 1, win[None, :]).astype(jnp.float32)
    pooled = (window_sum / count[None, :, :, None] - uf).astype(u.dtype)
    mixed = jnp.einsum('bsgc,gcd->bsgd', pooled, w_pool) + b_pool
    return mixed.reshape(Bn, S, POOL_WIDTH) * pool_scale


def attn_pool_mixer(x, w_in, w_pool, b_pool, pool_scale, w_out):
    Bn, S, _ = x.shape
    h = x @ w_in
    q, k, v, u = jnp.split(h, [ATT_WIDTH, 2 * ATT_WIDTH, 3 * ATT_WIDTH], axis=-1)
    shp = (Bn, S, ATT_HEADS, HEAD_DIM)
    a = moba_attention(q.reshape(shp), k.reshape(shp), v.reshape(shp))
    p = multiscale_pool(u, w_pool.reshape(POOL_GROUPS, POOL_GROUP_DIM, POOL_GROUP_DIM),
                        b_pool.reshape(POOL_GROUPS, POOL_GROUP_DIM), pool_scale)
    return jnp.concatenate([a, p], axis=-1) @ w_out


def short_conv_mixer(x, w_in, conv_w, conv_b, w_out):
    b_gate, c_gate, hval = jnp.split(x @ w_in, 3, axis=-1)
    return (b_gate * causal_dwconv(c_gate * hval, conv_w, conv_b)) @ w_out


def conv_glu_ffn(x, w_up, conv_w, conv_b, w_down):
    gu = causal_dwconv(x @ w_up, conv_w, conv_b)
    g, u = jnp.split(gu, 2, axis=-1)
    return (jax.nn.silu(g) * u) @ w_down


def setup_inputs(seed: int = 0) -> dict:
    key = jax.random.key(seed)
    ks = jax.random.split(key, 24)
    f32 = jnp.float32
    nrm = lambda k, shp, s: jax.random.normal(k, shp, f32) * s
    L = DEPTH
    return {
        "x": nrm(ks[0], (BATCH, SEQ, D_MODEL), 1.0),
        "even_w_in": nrm(ks[1], (N_EVEN, D_MODEL, EVEN_IN), D_MODEL ** -0.5),
        "even_w_pool": nrm(ks[2], (N_EVEN, POOL_GROUPS * POOL_GROUP_DIM, POOL_GROUP_DIM), POOL_GROUP_DIM ** -0.5),
        "even_b_pool": nrm(ks[3], (N_EVEN, POOL_WIDTH), 0.02),
        "even_pool_scale": 1.0 + nrm(ks[4], (N_EVEN, POOL_WIDTH), 0.02),
        "even_w_out": nrm(ks[5], (N_EVEN, D_MODEL, D_MODEL), DN_BETA * D_MODEL ** -0.5),
        "odd_w_in": nrm(ks[6], (N_ODD, D_MODEL, ODD_IN), D_MODEL ** -0.5),
        "odd_conv_w": nrm(ks[7], (N_ODD, CONV_WIDTH, D_MODEL), CONV_WIDTH ** -0.5),
        "odd_conv_b": nrm(ks[8], (N_ODD, D_MODEL), 0.02),
        "odd_w_out": nrm(ks[9], (N_ODD, D_MODEL, D_MODEL), DN_BETA * D_MODEL ** -0.5),
        "mix_ln_g": 1.0 + nrm(ks[10], (L, D_MODEL), 0.02),
        "mix_ln_b": nrm(ks[11], (L, D_MODEL), 0.02),
        "ffn_w_up": nrm(ks[12], (L, D_MODEL, 2 * D_FF), D_MODEL ** -0.5),
        "ffn_conv_w": nrm(ks[13], (L, CONV_WIDTH, 2 * D_FF), CONV_WIDTH ** -0.5),
        "ffn_conv_b": nrm(ks[14], (L, 2 * D_FF), 0.02),
        "ffn_w_down": nrm(ks[15], (L, D_FF, D_MODEL), DN_BETA * D_FF ** -0.5),
        "ffn_ln_g": 1.0 + nrm(ks[16], (L, D_MODEL), 0.02),
        "ffn_ln_b": nrm(ks[17], (L, D_MODEL), 0.02),
    }


def reference(x, even_w_in, even_w_pool, even_b_pool, even_pool_scale, even_w_out,
              odd_w_in, odd_conv_w, odd_conv_b, odd_w_out,
              mix_ln_g, mix_ln_b, ffn_w_up, ffn_conv_w, ffn_conv_b, ffn_w_down,
              ffn_ln_g, ffn_ln_b):
    for l in range(DEPTH):
        i = l // 2
        if l % 2 == 0:
            m = attn_pool_mixer(x, even_w_in[i], even_w_pool[i], even_b_pool[i],
                                even_pool_scale[i], even_w_out[i])
        else:
            m = short_conv_mixer(x, odd_w_in[i], odd_conv_w[i], odd_conv_b[i], odd_w_out[i])
        x = layer_norm(DN_ALPHA * x + m, mix_ln_g[l], mix_ln_b[l])
        f = conv_glu_ffn(x, ffn_w_up[l], ffn_conv_w[l], ffn_conv_b[l], ffn_w_down[l])
        x = layer_norm(DN_ALPHA * x + f, ffn_ln_g[l], ffn_ln_b[l])
    return x
```

```python
import functools

import jax
import jax.numpy as jnp
from jax import lax
from jax.experimental import pallas as pl
from jax.experimental.pallas import tpu as pltpu

F32 = jnp.float32
BF16 = jnp.bfloat16

D_MODEL = 1024
DEPTH = 4
ATT_HEADS = 8
HEAD_DIM = 64
ATT_WIDTH = ATT_HEADS * HEAD_DIM
MOBA_BLOCK = 256
MOBA_TOPK = 3
POOL_WINDOWS = (2, 4, 8, 16)
POOL_GROUP_DIM = 128
POOL_WIDTH = D_MODEL - ATT_WIDTH
CONV_WIDTH = 3
D_FF = 2816
DN_ALPHA = (2 * DEPTH) ** 0.25
LN_EPS = 1e-5
NEG = -1e30

LANES = 128
SUBLANES = 8
ROW_TILE = 512
COL_CHUNK = 256
POOL_HALO = 16
VMEM_LIMIT = 60000 * 1024


def _layer_norm(z, g, b):
    mu = jnp.mean(z, axis=-1, keepdims=True)
    zc = z - mu
    var = jnp.mean(zc * zc, axis=-1, keepdims=True)
    return zc * lax.rsqrt(var + LN_EPS) * g + b


def _causal_conv3(buf_ref, slot, carry_ref, idx, cur, w, rows):
    buf_ref[slot, 0:SUBLANES, :] = carry_ref[idx]
    buf_ref[slot, SUBLANES:SUBLANES + rows, :] = cur
    carry_ref[idx] = cur[rows - SUBLANES:rows, :]
    m2 = buf_ref[slot, SUBLANES - 2:SUBLANES - 2 + rows, :]
    m1 = buf_ref[slot, SUBLANES - 1:SUBLANES - 1 + rows, :]
    return w[0:1, :] * m2 + w[1:2, :] * m1 + w[2:3, :] * cur


def _ffn_kernel(x_ref, wup_ref, cw_ref, cb_ref, wdn_ref, g_ref, b_ref, o_ref,
                act_ref, buf_ref, carry_ref):
    rows = x_ref.shape[1]
    n_chunks = D_FF // COL_CHUNK

    @pl.when(pl.program_id(1) == 0)
    def _():
        carry_ref[...] = jnp.zeros_like(carry_ref)

    x = x_ref[0]
    xb = x.astype(BF16)
    for c in range(n_chunks):
        ys = []
        for half in range(2):
            col = half * D_FF + c * COL_CHUNK
            idx = half * n_chunks + c
            cur = jnp.dot(xb, wup_ref[:, col:col + COL_CHUNK], preferred_element_type=F32)
            y = _causal_conv3(buf_ref, idx % buf_ref.shape[0], carry_ref, idx, cur,
                              cw_ref[:, col:col + COL_CHUNK], rows)
            ys.append(y + cb_ref[:, col:col + COL_CHUNK])
        gate, up = ys
        act = gate * (1.0 / (1.0 + jnp.exp(-gate))) * up
        act_ref[:, c * COL_CHUNK:(c + 1) * COL_CHUNK] = act.astype(BF16)
    f = jnp.dot(act_ref[...], wdn_ref[...], preferred_element_type=F32)
    o_ref[0] = _layer_norm(DN_ALPHA * x + f, g_ref[...], b_ref[...])


def _const_spec(shape):
    nd = len(shape)
    return pl.BlockSpec(shape, lambda b, s: (0,) * nd, pipeline_mode=pl.Buffered(1))


def _row_spec(rows, width):
    return pl.BlockSpec((1, rows, width), lambda b, s: (b, s, 0))


def _params():
    return pltpu.CompilerParams(dimension_semantics=("arbitrary", "arbitrary"),
                                vmem_limit_bytes=VMEM_LIMIT)


def _ffn_call(x, w_up, conv_w, conv_b, w_down, ln_g, ln_b):
    bn, seq, d = x.shape
    rows = ROW_TILE
    n_chunks = D_FF // COL_CHUNK
    return pl.pallas_call(
        _ffn_kernel,
        grid=(bn, seq // rows),
        in_specs=[
            _row_spec(rows, d),
            _const_spec((d, 2 * D_FF)),
            _const_spec((CONV_WIDTH, 2 * D_FF)),
            _const_spec((1, 2 * D_FF)),
            _const_spec((D_FF, d)),
            _const_spec((1, d)),
            _const_spec((1, d)),
        ],
        out_specs=_row_spec(rows, d),
        out_shape=jax.ShapeDtypeStruct(x.shape, F32),
        scratch_shapes=[
            pltpu.VMEM((rows, D_FF), BF16),
            pltpu.VMEM((4, SUBLANES + rows, COL_CHUNK), F32),
            pltpu.VMEM((2 * n_chunks, SUBLANES, COL_CHUNK), F32),
        ],
        compiler_params=_params(),
        name="ffn",
    )(x, w_up, conv_w, conv_b, w_down, ln_g, ln_b)


def _odd_kernel(x_ref, win_ref, cw_ref, cb_ref, wout_ref, g_ref, b_ref, o_ref,
                y_ref, buf_ref, carry_ref):
    rows = x_ref.shape[1]
    d = x_ref.shape[2]
    n_chunks = d // COL_CHUNK

    @pl.when(pl.program_id(1) == 0)
    def _():
        carry_ref[...] = jnp.zeros_like(carry_ref)

    x = x_ref[0]
    xb = x.astype(BF16)
    for c in range(n_chunks):
        lo = c * COL_CHUNK
        hi = lo + COL_CHUNK
        b_gate = jnp.dot(xb, win_ref[:, lo:hi], preferred_element_type=F32)
        c_gate = jnp.dot(xb, win_ref[:, d + lo:d + hi], preferred_element_type=F32)
        hval = jnp.dot(xb, win_ref[:, 2 * d + lo:2 * d + hi], preferred_element_type=F32)
        conv = _causal_conv3(buf_ref, c % buf_ref.shape[0], carry_ref, c, c_gate * hval,
                             cw_ref[:, lo:hi], rows)
        y_ref[:, lo:hi] = (b_gate * (conv + cb_ref[:, lo:hi])).astype(BF16)
    m = jnp.dot(y_ref[...], wout_ref[...], preferred_element_type=F32)
    o_ref[0] = _layer_norm(DN_ALPHA * x + m, g_ref[...], b_ref[...])


def _odd_call(x, w_in, conv_w, conv_b, w_out, ln_g, ln_b):
    bn, seq, d = x.shape
    rows = ROW_TILE
    return pl.pallas_call(
        _odd_kernel,
        grid=(bn, seq // rows),
        in_specs=[
            _row_spec(rows, d),
            _const_spec((d, 3 * d)),
            _const_spec((CONV_WIDTH, d)),
            _const_spec((1, d)),
            _const_spec((d, d)),
            _const_spec((1, d)),
            _const_spec((1, d)),
        ],
        out_specs=_row_spec(rows, d),
        out_shape=jax.ShapeDtypeStruct(x.shape, F32),
        scratch_shapes=[
            pltpu.VMEM((rows, d), BF16),
            pltpu.VMEM((2, SUBLANES + rows, COL_CHUNK), F32),
            pltpu.VMEM((d // COL_CHUNK, SUBLANES, COL_CHUNK), F32),
        ],
        compiler_params=_params(),
        name="odd_mixer",
    )(x, w_in, conv_w, conv_b, w_out, ln_g, ln_b)


def _even_in_kernel(x_ref, win_ref, wpool_ref, bpool_ref, pscale_ref,
                    q_ref, k_ref, vt_ref, kmean_ref, p_ref, ubuf_ref):
    rows = x_ref.shape[1]
    blocks = rows // MOBA_BLOCK
    s = pl.program_id(1)
    xb = x_ref[0].astype(BF16)

    q = jnp.dot(xb, win_ref[:, 0:ATT_WIDTH], preferred_element_type=F32)
    q_ref[0] = (q * (HEAD_DIM ** -0.5)).astype(BF16)

    k = jnp.dot(xb, win_ref[:, ATT_WIDTH:2 * ATT_WIDTH], preferred_element_type=F32)
    v = jnp.dot(xb, win_ref[:, 2 * ATT_WIDTH:3 * ATT_WIDTH], preferred_element_type=F32)
    for r in range(blocks):
        kb = k[r * MOBA_BLOCK:(r + 1) * MOBA_BLOCK, :]
        k_ref[0, r] = kb.astype(BF16)
        kmean_ref[0, pl.ds(s * blocks + r, 1), :] = (
            jnp.sum(kb, axis=0, keepdims=True) * (1.0 / MOBA_BLOCK))
        vt_ref[0, r] = v[r * MOBA_BLOCK:(r + 1) * MOBA_BLOCK, :].T.astype(BF16)

    u = jnp.dot(xb, win_ref[:, 3 * ATT_WIDTH:3 * ATT_WIDTH + POOL_WIDTH],
                preferred_element_type=F32)

    @pl.when(s == 0)
    def _():
        ubuf_ref[0:POOL_HALO, :] = jnp.zeros((POOL_HALO, POOL_WIDTH), F32)

    @pl.when(s != 0)
    def _():
        ubuf_ref[0:POOL_HALO, :] = ubuf_ref[rows:rows + POOL_HALO, :]

    ubuf_ref[POOL_HALO:POOL_HALO + rows, :] = u
    t = s * rows + lax.broadcasted_iota(jnp.int32, (rows, POOL_GROUP_DIM), 0)
    for g, win in enumerate(POOL_WINDOWS):
        lo = g * POOL_GROUP_DIM
        hi = lo + POOL_GROUP_DIM
        ug = u[:, lo:hi]
        wsum = ug
        for dlt in range(1, win):
            wsum = wsum + ubuf_ref[POOL_HALO - dlt:POOL_HALO - dlt + rows, lo:hi]
        count = jnp.minimum(t + 1, win).astype(F32)
        pooled = (wsum / count - ug).astype(BF16)
        mixed = jnp.dot(pooled, wpool_ref[g], preferred_element_type=F32) + bpool_ref[:, lo:hi]
        p_ref[0, :, lo:hi] = (mixed * pscale_ref[:, lo:hi]).astype(BF16)


def _even_in_call(x, w_in, w_pool, b_pool, pool_scale):
    bn, seq, d = x.shape
    rows = ROW_TILE
    nb = seq // MOBA_BLOCK
    blocks = rows // MOBA_BLOCK
    groups = len(POOL_WINDOWS)
    return pl.pallas_call(
        _even_in_kernel,
        grid=(bn, seq // rows),
        in_specs=[
            _row_spec(rows, d),
            _const_spec((d, 3 * ATT_WIDTH + POOL_WIDTH)),
            _const_spec((groups, POOL_GROUP_DIM, POOL_GROUP_DIM)),
            _const_spec((1, POOL_WIDTH)),
            _const_spec((1, POOL_WIDTH)),
        ],
        out_specs=[
            _row_spec(rows, ATT_WIDTH),
            pl.BlockSpec((1, blocks, MOBA_BLOCK, ATT_WIDTH), lambda b, s: (b, s, 0, 0)),
            pl.BlockSpec((1, blocks, ATT_WIDTH, MOBA_BLOCK), lambda b, s: (b, s, 0, 0)),
            pl.BlockSpec((1, nb, ATT_WIDTH), lambda b, s: (b, 0, 0)),
            _row_spec(rows, POOL_WIDTH),
        ],
        out_shape=[
            jax.ShapeDtypeStruct((bn, seq, ATT_WIDTH), BF16),
            jax.ShapeDtypeStruct((bn, nb, MOBA_BLOCK, ATT_WIDTH), BF16),
            jax.ShapeDtypeStruct((bn, nb, ATT_WIDTH, MOBA_BLOCK), BF16),
            jax.ShapeDtypeStruct((bn, nb, ATT_WIDTH), F32),
            jax.ShapeDtypeStruct((bn, seq, POOL_WIDTH), BF16),
        ],
        scratch_shapes=[pltpu.VMEM((POOL_HALO + rows, POOL_WIDTH), F32)],
        compiler_params=_params(),
        name="even_in",
    )(x, w_in, w_pool, b_pool, pool_scale)


def _attn_kernel(x_ref, q_ref, k_ref, vt_ref, kmean_ref, p_ref, wout_ref, g_ref, b_ref,
                 o_ref, qm_ref, sel_ref, m_ref, l_ref, acc_ref):
    blk = pl.program_id(1)
    nb = kmean_ref.shape[1]
    bs = MOBA_BLOCK
    pair = 2 * HEAD_DIM
    contract_last = (((1,), (1,)), ((), ()))

    m_ref[...] = jnp.full(m_ref.shape, NEG, F32)
    l_ref[...] = jnp.zeros(l_ref.shape, F32)
    acc_ref[...] = jnp.zeros(acc_ref.shape, F32)

    lane = lax.broadcasted_iota(jnp.int32, (bs, pair), 1)
    row_id = lax.broadcasted_iota(jnp.int32, (nb, bs), 0)
    k_top = jnp.minimum(MOBA_TOPK, blk)
    for h in range(ATT_HEADS):
        j, e = divmod(h, 2)
        q_pair = q_ref[0, :, j * pair:(j + 1) * pair]
        qm = jnp.where((lane >= e * HEAD_DIM) & (lane < (e + 1) * HEAD_DIM), q_pair,
                       jnp.zeros_like(q_pair))
        qm_ref[h] = qm
        km = kmean_ref[0, :, j * pair:(j + 1) * pair].astype(BF16)
        gate = lax.dot_general(km, qm, contract_last, preferred_element_type=F32)
        gate = jnp.where(row_id < blk, gate, NEG)
        rank = jnp.zeros((nb, bs), jnp.int32)
        for n2 in range(nb):
            gn = gate[n2:n2 + 1, :]
            beats = (gn > gate) | ((gn == gate) & (row_id > n2))
            rank = rank + beats.astype(jnp.int32)
        sel_ref[h * nb:(h + 1) * nb, :] = (rank < k_top).astype(F32)

    def process(h, k_blk, vt_blk, mask):
        s_t = lax.dot_general(k_blk, qm_ref[h], contract_last, preferred_element_type=F32)
        s_t = jnp.where(mask, s_t, NEG)
        r = h * SUBLANES
        m_old = m_ref[r:r + 1, :]
        m_new = jnp.maximum(m_old, jnp.max(s_t, axis=0, keepdims=True))
        alpha = jnp.exp(m_old - m_new)
        p_t = jnp.exp(s_t - m_new)
        l_ref[r:r + 1, :] = alpha * l_ref[r:r + 1, :] + jnp.sum(p_t, axis=0, keepdims=True)
        pv = jnp.dot(vt_blk, p_t.astype(BF16), preferred_element_type=F32)
        lo = h * HEAD_DIM
        acc_ref[lo:lo + HEAD_DIM, :] = alpha * acc_ref[lo:lo + HEAD_DIM, :] + pv
        m_ref[r:r + 1, :] = m_new

    key_id = lax.broadcasted_iota(jnp.int32, (bs, bs), 0)
    qry_id = lax.broadcasted_iota(jnp.int32, (bs, bs), 1)
    causal = key_id <= qry_id
    for h in range(ATT_HEADS):
        j = h // 2
        process(h, k_ref[0, blk, :, j * pair:(j + 1) * pair],
                vt_ref[0, blk, h * HEAD_DIM:(h + 1) * HEAD_DIM, :], causal)

    def past(n, carry):
        for h in range(ATT_HEADS):
            j = h // 2
            mask = sel_ref[pl.ds(h * nb + n, 1), :] > 0.5
            process(h, k_ref[0, n, :, j * pair:(j + 1) * pair],
                    vt_ref[0, n, h * HEAD_DIM:(h + 1) * HEAD_DIM, :], mask)
        return carry

    lax.fori_loop(0, blk, past, 0)

    for h in range(ATT_HEADS):
        r = h * SUBLANES
        lo = h * HEAD_DIM
        acc_ref[lo:lo + HEAD_DIM, :] = acc_ref[lo:lo + HEAD_DIM, :] * (1.0 / l_ref[r:r + 1, :])
    a = acc_ref[...].T.astype(BF16)
    m = (jnp.dot(a, wout_ref[0:ATT_WIDTH, :], preferred_element_type=F32)
         + jnp.dot(p_ref[0], wout_ref[ATT_WIDTH:, :], preferred_element_type=F32))
    o_ref[0] = _layer_norm(DN_ALPHA * x_ref[0] + m, g_ref[...], b_ref[...])


def _attn_call(x, q, k, vt, kmean, p, w_out, ln_g, ln_b):
    bn, seq, d = x.shape
    nb = seq // MOBA_BLOCK
    rows = MOBA_BLOCK
    return pl.pallas_call(
        _attn_kernel,
        grid=(bn, nb),
        in_specs=[
            _row_spec(rows, d),
            _row_spec(rows, ATT_WIDTH),
            pl.BlockSpec((1, nb, MOBA_BLOCK, ATT_WIDTH), lambda b, s: (b, 0, 0, 0)),
            pl.BlockSpec((1, nb, ATT_WIDTH, MOBA_BLOCK), lambda b, s: (b, 0, 0, 0)),
            pl.BlockSpec((1, nb, ATT_WIDTH), lambda b, s: (b, 0, 0)),
            _row_spec(rows, POOL_WIDTH),
            _const_spec((d, d)),
            _const_spec((1, d)),
            _const_spec((1, d)),
        ],
        out_specs=_row_spec(rows, d),
        out_shape=jax.ShapeDtypeStruct(x.shape, F32),
        scratch_shapes=[
            pltpu.VMEM((ATT_HEADS, MOBA_BLOCK, 2 * HEAD_DIM), BF16),
            pltpu.VMEM((ATT_HEADS * nb, MOBA_BLOCK), F32),
            pltpu.VMEM((ATT_HEADS * SUBLANES, MOBA_BLOCK), F32),
            pltpu.VMEM((ATT_HEADS * SUBLANES, MOBA_BLOCK), F32),
            pltpu.VMEM((ATT_WIDTH, MOBA_BLOCK), F32),
        ],
        compiler_params=_params(),
        name="moba_attn",
    )(x, q, k, vt, kmean, p, w_out, ln_g, ln_b)


def kernel(x, even_w_in, even_w_pool, even_b_pool, even_pool_scale, even_w_out, odd_w_in, odd_conv_w, odd_conv_b, odd_w_out, mix_ln_g, mix_ln_b, ffn_w_up, ffn_conv_w, ffn_conv_b, ffn_w_down, ffn_ln_g, ffn_ln_b):
    groups = len(POOL_WINDOWS)
    row = lambda a: a.reshape(1, -1)
    for l in range(DEPTH):
        i = l // 2
        if l % 2 == 0:
            q, k, vt, kmean, p = _even_in_call(
                x, even_w_in[i].astype(BF16),
                even_w_pool[i].reshape(groups, POOL_GROUP_DIM, POOL_GROUP_DIM).astype(BF16),
                row(even_b_pool[i]), row(even_pool_scale[i]))
            x = _attn_call(x, q, k, vt, kmean, p, even_w_out[i].astype(BF16),
                           row(mix_ln_g[l]), row(mix_ln_b[l]))
        else:
            x = _odd_call(x, odd_w_in[i].astype(BF16), odd_conv_w[i], row(odd_conv_b[i]),
                          odd_w_out[i].astype(BF16), row(mix_ln_g[l]), row(mix_ln_b[l]))
        x = _ffn_call(x, ffn_w_up[l].astype(BF16), ffn_conv_w[l], row(ffn_conv_b[l]),
                      ffn_w_down[l].astype(BF16), row(ffn_ln_g[l]), row(ffn_ln_b[l]))
    return x
```

```python
import functools

import jax
import jax.numpy as jnp
from jax import lax
from jax.experimental import pallas as pl
from jax.experimental.pallas import tpu as pltpu

F32 = jnp.float32
BF16 = jnp.bfloat16

D_MODEL = 1024
DEPTH = 4
ATT_HEADS = 8
HEAD_DIM = 64
ATT_WIDTH = ATT_HEADS * HEAD_DIM
MOBA_BLOCK = 256
MOBA_TOPK = 3
POOL_WINDOWS = (2, 4, 8, 16)
POOL_GROUP_DIM = 128
POOL_WIDTH = D_MODEL - ATT_WIDTH
CONV_WIDTH = 3
D_FF = 2816
DN_ALPHA = (2 * DEPTH) ** 0.25
LN_EPS = 1e-5
NEG = -1e30
LOG2_E = 1.4426950408889634

LANES = 128
SUBLANES = 8
ROW_TILE = 512
COL_CHUNK = 256
POOL_HALO = 16
VMEM_LIMIT = 60000 * 1024


def _layer_norm(z, g, b):
    mu = jnp.mean(z, axis=-1, keepdims=True)
    zc = z - mu
    var = jnp.mean(zc * zc, axis=-1, keepdims=True)
    return zc * lax.rsqrt(var + LN_EPS) * g + b


def _causal_conv3(buf_ref, slot, carry_ref, idx, cur, w, rows):
    buf_ref[slot, 0:SUBLANES, :] = carry_ref[idx]
    buf_ref[slot, SUBLANES:SUBLANES + rows, :] = cur
    carry_ref[idx] = cur[rows - SUBLANES:rows, :]
    m2 = buf_ref[slot, SUBLANES - 2:SUBLANES - 2 + rows, :]
    m1 = buf_ref[slot, SUBLANES - 1:SUBLANES - 1 + rows, :]
    return w[0:1, :] * m2 + w[1:2, :] * m1 + w[2:3, :] * cur


def _ffn_kernel(x_ref, wup_ref, cw_ref, cb_ref, wdn_ref, g_ref, b_ref, o_ref,
                act_ref, buf_ref, carry_ref):
    rows = x_ref.shape[1]
    n_chunks = D_FF // COL_CHUNK

    @pl.when(pl.program_id(1) == 0)
    def _():
        carry_ref[...] = jnp.zeros_like(carry_ref)

    x = x_ref[0]
    xb = x.astype(BF16)
    for c in range(n_chunks):
        ys = []
        for half in range(2):
            col = half * D_FF + c * COL_CHUNK
            idx = half * n_chunks + c
            cur = jnp.dot(xb, wup_ref[:, col:col + COL_CHUNK], preferred_element_type=F32)
            y = _causal_conv3(buf_ref, idx % buf_ref.shape[0], carry_ref, idx, cur,
                              cw_ref[:, col:col + COL_CHUNK], rows)
            ys.append(y + cb_ref[:, col:col + COL_CHUNK])
        gate, up = ys
        act = gate * (1.0 / (1.0 + jnp.exp(-gate))) * up
        act_ref[:, c * COL_CHUNK:(c + 1) * COL_CHUNK] = act.astype(BF16)
    f = jnp.dot(act_ref[...], wdn_ref[...], preferred_element_type=F32)
    o_ref[0] = _layer_norm(DN_ALPHA * x + f, g_ref[...], b_ref[...])


def _const_spec(shape):
    nd = len(shape)
    return pl.BlockSpec(shape, lambda b, s: (0,) * nd, pipeline_mode=pl.Buffered(1))


def _row_spec(rows, width):
    return pl.BlockSpec((1, rows, width), lambda b, s: (b, s, 0))


def _params():
    return pltpu.CompilerParams(dimension_semantics=("arbitrary", "arbitrary"),
                                vmem_limit_bytes=VMEM_LIMIT)


def _ffn_call(x, w_up, conv_w, conv_b, w_down, ln_g, ln_b):
    bn, seq, d = x.shape
    rows = ROW_TILE
    n_chunks = D_FF // COL_CHUNK
    return pl.pallas_call(
        _ffn_kernel,
        grid=(bn, seq // rows),
        in_specs=[
            _row_spec(rows, d),
            _const_spec((d, 2 * D_FF)),
            _const_spec((CONV_WIDTH, 2 * D_FF)),
            _const_spec((1, 2 * D_FF)),
            _const_spec((D_FF, d)),
            _const_spec((1, d)),
            _const_spec((1, d)),
        ],
        out_specs=_row_spec(rows, d),
        out_shape=jax.ShapeDtypeStruct(x.shape, F32),
        scratch_shapes=[
            pltpu.VMEM((rows, D_FF), BF16),
            pltpu.VMEM((4, SUBLANES + rows, COL_CHUNK), F32),
            pltpu.VMEM((2 * n_chunks, SUBLANES, COL_CHUNK), F32),
        ],
        compiler_params=_params(),
        name="ffn",
    )(x, w_up, conv_w, conv_b, w_down, ln_g, ln_b)


def _odd_kernel(x_ref, win_ref, cw_ref, cb_ref, wout_ref, g_ref, b_ref, o_ref,
                y_ref, buf_ref, carry_ref):
    rows = x_ref.shape[1]
    d = x_ref.shape[2]
    n_chunks = d // COL_CHUNK

    @pl.when(pl.program_id(1) == 0)
    def _():
        carry_ref[...] = jnp.zeros_like(carry_ref)

    x = x_ref[0]
    xb = x.astype(BF16)
    for c in range(n_chunks):
        lo = c * COL_CHUNK
        hi = lo + COL_CHUNK
        b_gate = jnp.dot(xb, win_ref[:, lo:hi], preferred_element_type=F32)
        c_gate = jnp.dot(xb, win_ref[:, d + lo:d + hi], preferred_element_type=F32)
        hval = jnp.dot(xb, win_ref[:, 2 * d + lo:2 * d + hi], preferred_element_type=F32)
        conv = _causal_conv3(buf_ref, c % buf_ref.shape[0], carry_ref, c, c_gate * hval,
                             cw_ref[:, lo:hi], rows)
        y_ref[:, lo:hi] = (b_gate * (conv + cb_ref[:, lo:hi])).astype(BF16)
    m = jnp.dot(y_ref[...], wout_ref[...], preferred_element_type=F32)
    o_ref[0] = _layer_norm(DN_ALPHA * x + m, g_ref[...], b_ref[...])


def _odd_call(x, w_in, conv_w, conv_b, w_out, ln_g, ln_b):
    bn, seq, d = x.shape
    rows = ROW_TILE
    return pl.pallas_call(
        _odd_kernel,
        grid=(bn, seq // rows),
        in_specs=[
            _row_spec(rows, d),
            _const_spec((d, 3 * d)),
            _const_spec((CONV_WIDTH, d)),
            _const_spec((1, d)),
            _const_spec((d, d)),
            _const_spec((1, d)),
            _const_spec((1, d)),
        ],
        out_specs=_row_spec(rows, d),
        out_shape=jax.ShapeDtypeStruct(x.shape, F32),
        scratch_shapes=[
            pltpu.VMEM((rows, d), BF16),
            pltpu.VMEM((2, SUBLANES + rows, COL_CHUNK), F32),
            pltpu.VMEM((d // COL_CHUNK, SUBLANES, COL_CHUNK), F32),
        ],
        compiler_params=_params(),
        name="odd_mixer",
    )(x, w_in, conv_w, conv_b, w_out, ln_g, ln_b)


def _even_in_kernel(x_ref, win_ref, wpool_ref, bpool_ref, pscale_ref,
                    q_ref, k_ref, vt_ref, kmean_ref, p_ref, ubuf_ref):
    rows = x_ref.shape[1]
    blocks = rows // MOBA_BLOCK
    s = pl.program_id(1)
    xb = x_ref[0].astype(BF16)

    q = jnp.dot(xb, win_ref[:, 0:ATT_WIDTH], preferred_element_type=F32)
    q_ref[0] = (q * (HEAD_DIM ** -0.5 * LOG2_E)).astype(BF16)

    k = jnp.dot(xb, win_ref[:, ATT_WIDTH:2 * ATT_WIDTH], preferred_element_type=F32)
    v = jnp.dot(xb, win_ref[:, 2 * ATT_WIDTH:3 * ATT_WIDTH], preferred_element_type=F32)
    for r in range(blocks):
        kb = k[r * MOBA_BLOCK:(r + 1) * MOBA_BLOCK, :]
        k_ref[0, r] = kb.astype(BF16)
        kmean_ref[0, pl.ds(s * blocks + r, 1), :] = (
            jnp.sum(kb, axis=0, keepdims=True) * (1.0 / MOBA_BLOCK))
        vt_ref[0, r] = v[r * MOBA_BLOCK:(r + 1) * MOBA_BLOCK, :].T.astype(BF16)

    u = jnp.dot(xb, win_ref[:, 3 * ATT_WIDTH:3 * ATT_WIDTH + POOL_WIDTH],
                preferred_element_type=F32)

    @pl.when(s == 0)
    def _():
        ubuf_ref[0:POOL_HALO, :] = jnp.zeros((POOL_HALO, POOL_WIDTH), F32)

    @pl.when(s != 0)
    def _():
        ubuf_ref[0:POOL_HALO, :] = ubuf_ref[rows:rows + POOL_HALO, :]

    ubuf_ref[POOL_HALO:POOL_HALO + rows, :] = u
    t = s * rows + lax.broadcasted_iota(jnp.int32, (rows, POOL_GROUP_DIM), 0)
    for g, win in enumerate(POOL_WINDOWS):
        lo = g * POOL_GROUP_DIM
        hi = lo + POOL_GROUP_DIM
        ug = u[:, lo:hi]
        wsum = ug
        for dlt in range(1, win):
            wsum = wsum + ubuf_ref[POOL_HALO - dlt:POOL_HALO - dlt + rows, lo:hi]
        count = jnp.minimum(t + 1, win).astype(F32)
        pooled = (wsum / count - ug).astype(BF16)
        mixed = jnp.dot(pooled, wpool_ref[g], preferred_element_type=F32) + bpool_ref[:, lo:hi]
        p_ref[0, :, lo:hi] = (mixed * pscale_ref[:, lo:hi]).astype(BF16)


def _even_in_call(x, w_in, w_pool, b_pool, pool_scale):
    bn, seq, d = x.shape
    rows = ROW_TILE
    nb = seq // MOBA_BLOCK
    blocks = rows // MOBA_BLOCK
    groups = len(POOL_WINDOWS)
    return pl.pallas_call(
        _even_in_kernel,
        grid=(bn, seq // rows),
        in_specs=[
            _row_spec(rows, d),
            _const_spec((d, 3 * ATT_WIDTH + POOL_WIDTH)),
            _const_spec((groups, POOL_GROUP_DIM, POOL_GROUP_DIM)),
            _const_spec((1, POOL_WIDTH)),
            _const_spec((1, POOL_WIDTH)),
        ],
        out_specs=[
            _row_spec(rows, ATT_WIDTH),
            pl.BlockSpec((1, blocks, MOBA_BLOCK, ATT_WIDTH), lambda b, s: (b, s, 0, 0)),
            pl.BlockSpec((1, blocks, ATT_WIDTH, MOBA_BLOCK), lambda b, s: (b, s, 0, 0)),
            pl.BlockSpec((1, nb, ATT_WIDTH), lambda b, s: (b, 0, 0)),
            _row_spec(rows, POOL_WIDTH),
        ],
        out_shape=[
            jax.ShapeDtypeStruct((bn, seq, ATT_WIDTH), BF16),
            jax.ShapeDtypeStruct((bn, nb, MOBA_BLOCK, ATT_WIDTH), BF16),
            jax.ShapeDtypeStruct((bn, nb, ATT_WIDTH, MOBA_BLOCK), BF16),
            jax.ShapeDtypeStruct((bn, nb, ATT_WIDTH), F32),
            jax.ShapeDtypeStruct((bn, seq, POOL_WIDTH), BF16),
        ],
        scratch_shapes=[pltpu.VMEM((POOL_HALO + rows, POOL_WIDTH), F32)],
        compiler_params=_params(),
        name="even_in",
    )(x, w_in, w_pool, b_pool, pool_scale)


def _attn_kernel(x_ref, q_ref, k_ref, vt_ref, kmean_ref, p_ref, wout_ref, g_ref, b_ref,
                 o_ref, qm_ref, sel_ref, m_ref, l_ref, acc_ref):
    blk = pl.program_id(1)
    nb = kmean_ref.shape[1]
    bs = MOBA_BLOCK
    pair = 2 * HEAD_DIM
    contract_last = (((1,), (1,)), ((), ()))

    m_ref[...] = jnp.full(m_ref.shape, NEG, F32)
    l_ref[...] = jnp.zeros(l_ref.shape, F32)
    acc_ref[...] = jnp.zeros(acc_ref.shape, F32)

    lane = lax.broadcasted_iota(jnp.int32, (bs, pair), 1)
    row_id = lax.broadcasted_iota(jnp.int32, (nb, bs), 0)
    k_top = jnp.minimum(MOBA_TOPK, blk)
    for h in range(ATT_HEADS):
        j, e = divmod(h, 2)
        q_pair = q_ref[0, :, j * pair:(j + 1) * pair]
        qm = jnp.where((lane >= e * HEAD_DIM) & (lane < (e + 1) * HEAD_DIM), q_pair,
                       jnp.zeros_like(q_pair))
        qm_ref[h] = qm
        km = kmean_ref[0, :, j * pair:(j + 1) * pair].astype(BF16)
        gate = lax.dot_general(km, qm, contract_last, preferred_element_type=F32)
        gate = jnp.where(row_id < blk, gate, NEG)
        rank = jnp.zeros((nb, bs), jnp.int32)
        for n2 in range(nb):
            gn = gate[n2:n2 + 1, :]
            beats = (gn > gate) | ((gn == gate) & (row_id > n2))
            rank = rank + beats.astype(jnp.int32)
        sel_ref[h * nb:(h + 1) * nb, :] = (rank < k_top).astype(F32)

    ones_rows = jnp.ones((2 * SUBLANES, bs), BF16)

    def attend(n, key_mask, query_sel):
        scores = []
        for h in range(ATT_HEADS):
            j = h // 2
            scores.append(lax.dot_general(k_ref[0, n, :, j * pair:(j + 1) * pair], qm_ref[h],
                                          contract_last, preferred_element_type=F32))
        probs, alphas = [], []
        for h in range(ATT_HEADS):
            s_t = scores[h] if key_mask is None else jnp.where(key_mask, scores[h], NEG)
            r = h * SUBLANES
            m_old = m_ref[r:r + 1, :]
            b_max = jnp.max(s_t, axis=0, keepdims=True)
            if query_sel is not None:
                sel = query_sel(h)
                b_max = jnp.where(sel, b_max, NEG)
            m_new = jnp.maximum(m_old, b_max)
            m_sub = m_new if query_sel is None else jnp.where(sel, m_new, -NEG)
            m_ref[r:r + 1, :] = m_new
            probs.append(jnp.exp2(s_t - m_sub).astype(BF16))
            alphas.append(jnp.exp2(m_old - m_new))
        for h in range(ATT_HEADS):
            lo = h * HEAD_DIM
            r = h * SUBLANES
            lhs = jnp.concatenate([vt_ref[0, n, lo:lo + HEAD_DIM, :], ones_rows], axis=0)
            pv = jnp.dot(lhs, probs[h], preferred_element_type=F32)
            acc_ref[lo:lo + HEAD_DIM, :] = alphas[h] * acc_ref[lo:lo + HEAD_DIM, :] + pv[0:HEAD_DIM, :]
            l_ref[r:r + 1, :] = alphas[h] * l_ref[r:r + 1, :] + pv[HEAD_DIM:HEAD_DIM + 1, :]

    key_id = lax.broadcasted_iota(jnp.int32, (bs, bs), 0)
    qry_id = lax.broadcasted_iota(jnp.int32, (bs, bs), 1)
    attend(blk, key_id <= qry_id, None)

    def past(n, carry):
        attend(n, None, lambda h: sel_ref[pl.ds(h * nb + n, 1), :] > 0.5)
        return carry

    lax.fori_loop(0, blk, past, 0)

    for h in range(ATT_HEADS):
        r = h * SUBLANES
        lo = h * HEAD_DIM
        acc_ref[lo:lo + HEAD_DIM, :] = acc_ref[lo:lo + HEAD_DIM, :] * (1.0 / l_ref[r:r + 1, :])
    a = acc_ref[...].T.astype(BF16)
    m = (jnp.dot(a, wout_ref[0:ATT_WIDTH, :], preferred_element_type=F32)
         + jnp.dot(p_ref[0], wout_ref[ATT_WIDTH:, :], preferred_element_type=F32))
    o_ref[0] = _layer_norm(DN_ALPHA * x_ref[0] + m, g_ref[...], b_ref[...])


def _attn_call(x, q, k, vt, kmean, p, w_out, ln_g, ln_b):
    bn, seq, d = x.shape
    nb = seq // MOBA_BLOCK
    rows = MOBA_BLOCK
    return pl.pallas_call(
        _attn_kernel,
        grid=(bn, nb),
        in_specs=[
            _row_spec(rows, d),
            _row_spec(rows, ATT_WIDTH),
            pl.BlockSpec((1, nb, MOBA_BLOCK, ATT_WIDTH), lambda b, s: (b, 0, 0, 0)),
            pl.BlockSpec((1, nb, ATT_WIDTH, MOBA_BLOCK), lambda b, s: (b, 0, 0, 0)),
            pl.BlockSpec((1, nb, ATT_WIDTH), lambda b, s: (b, 0, 0)),
            _row_spec(rows, POOL_WIDTH),
            _const_spec((d, d)),
            _const_spec((1, d)),
            _const_spec((1, d)),
        ],
        out_specs=_row_spec(rows, d),
        out_shape=jax.ShapeDtypeStruct(x.shape, F32),
        scratch_shapes=[
            pltpu.VMEM((ATT_HEADS, MOBA_BLOCK, 2 * HEAD_DIM), BF16),
            pltpu.VMEM((ATT_HEADS * nb, MOBA_BLOCK), F32),
            pltpu.VMEM((ATT_HEADS * SUBLANES, MOBA_BLOCK), F32),
            pltpu.VMEM((ATT_HEADS * SUBLANES, MOBA_BLOCK), F32),
            pltpu.VMEM((ATT_WIDTH, MOBA_BLOCK), F32),
        ],
        compiler_params=_params(),
        name="moba_attn",
    )(x, q, k, vt, kmean, p, w_out, ln_g, ln_b)


def kernel(x, even_w_in, even_w_pool, even_b_pool, even_pool_scale, even_w_out, odd_w_in, odd_conv_w, odd_conv_b, odd_w_out, mix_ln_g, mix_ln_b, ffn_w_up, ffn_conv_w, ffn_conv_b, ffn_w_down, ffn_ln_g, ffn_ln_b):
    groups = len(POOL_WINDOWS)
    row = lambda a: a.reshape(1, -1)
    for l in range(DEPTH):
        i = l // 2
        if l % 2 == 0:
            q, k, vt, kmean, p = _even_in_call(
                x, even_w_in[i].astype(BF16),
                even_w_pool[i].reshape(groups, POOL_GROUP_DIM, POOL_GROUP_DIM).astype(BF16),
                row(even_b_pool[i]), row(even_pool_scale[i]))
            x = _attn_call(x, q, k, vt, kmean, p, even_w_out[i].astype(BF16),
                           row(mix_ln_g[l]), row(mix_ln_b[l]))
        else:
            x = _odd_call(x, odd_w_in[i].astype(BF16), odd_conv_w[i], row(odd_conv_b[i]),
                          odd_w_out[i].astype(BF16), row(mix_ln_g[l]), row(mix_ln_b[l]))
        x = _ffn_call(x, ffn_w_up[l].astype(BF16), ffn_conv_w[l], row(ffn_conv_b[l]),
                      ffn_w_down[l].astype(BF16), row(ffn_ln_g[l]), row(ffn_ln_b[l]))
    return x
```

```python
import jax
import jax.numpy as jnp
from jax import lax
from jax.experimental import pallas as pl
from jax.experimental.pallas import tpu as pltpu

F32 = jnp.float32
BF16 = jnp.bfloat16

D_MODEL = 1024
DEPTH = 4
ATT_HEADS = 8
HEAD_DIM = 64
ATT_WIDTH = ATT_HEADS * HEAD_DIM
MOBA_BLOCK = 256
MOBA_TOPK = 3
POOL_WINDOWS = (2, 4, 8, 16)
POOL_GROUP_DIM = 128
POOL_WIDTH = D_MODEL - ATT_WIDTH
CONV_WIDTH = 3
D_FF = 2816
DN_ALPHA = (2 * DEPTH) ** 0.25
LN_EPS = 1e-5
NEG = -1e30
LOG2_E = 1.4426950408889634

LANES = 128
SUBLANES = 8
ROW_TILE = 512
COL_CHUNK = 256
POOL_HALO = 16
VMEM_LIMIT = 60000 * 1024


def _layer_norm(z, g, b):
    mu = jnp.mean(z, axis=-1, keepdims=True)
    zc = z - mu
    var = jnp.mean(zc * zc, axis=-1, keepdims=True)
    return zc * lax.rsqrt(var + LN_EPS) * g + b


def _shift_rows(v, prev_tail):
    rolled = pltpu.roll(v, 1, 0)
    first = lax.broadcasted_iota(jnp.int32, (SUBLANES, v.shape[1]), 0) == 0
    head = jnp.where(first, pltpu.roll(prev_tail, 1, 0), rolled[0:SUBLANES, :])
    return jnp.concatenate([head, rolled[SUBLANES:, :]], axis=0)


def _causal_conv3(carry_ref, idx, cur, w, rows):
    a = w[0:1, :] * cur
    b = w[1:2, :] * cur + _shift_rows(a, carry_ref[2 * idx])
    y = w[2:3, :] * cur + _shift_rows(b, carry_ref[2 * idx + 1])
    carry_ref[2 * idx] = a[rows - SUBLANES:rows, :]
    carry_ref[2 * idx + 1] = b[rows - SUBLANES:rows, :]
    return y


def _ffn_kernel(x_ref, wup_ref, cw_ref, cb_ref, wdn_ref, g_ref, b_ref, o_ref,
                act_ref, carry_ref):
    rows = x_ref.shape[1]
    n_chunks = D_FF // COL_CHUNK

    @pl.when(pl.program_id(1) == 0)
    def _():
        carry_ref[...] = jnp.zeros_like(carry_ref)

    x = x_ref[0]
    xb = x.astype(BF16)
    for c in range(n_chunks):
        ys = []
        for half in range(2):
            col = half * D_FF + c * COL_CHUNK
            cur = jnp.dot(xb, wup_ref[:, col:col + COL_CHUNK], preferred_element_type=F32)
            y = _causal_conv3(carry_ref, half * n_chunks + c, cur, cw_ref[:, col:col + COL_CHUNK], rows)
            ys.append(y + cb_ref[:, col:col + COL_CHUNK])
        gate, up = ys
        act = gate * (1.0 / (1.0 + jnp.exp(-gate))) * up
        act_ref[:, c * COL_CHUNK:(c + 1) * COL_CHUNK] = act.astype(BF16)
    f = jnp.dot(act_ref[...], wdn_ref[...], preferred_element_type=F32)
    o_ref[0] = _layer_norm(DN_ALPHA * x + f, g_ref[...], b_ref[...])


def _layer_spec(layer, shape):
    nd = len(shape)
    return pl.BlockSpec((None,) + tuple(shape), lambda b, s: (layer,) + (0,) * nd,
                        pipeline_mode=pl.Buffered(1))


def _row_spec(rows, width):
    return pl.BlockSpec((1, rows, width), lambda b, s: (b, s, 0))


def _params(flags=None):
    return pltpu.CompilerParams(dimension_semantics=("arbitrary", "arbitrary"),
                                vmem_limit_bytes=VMEM_LIMIT, flags=flags)


def _ffn_call(x, layer, w_up, conv_w, conv_b, w_down, ln_g, ln_b):
    bn, seq, d = x.shape
    rows = ROW_TILE
    n_chunks = D_FF // COL_CHUNK
    return pl.pallas_call(
        _ffn_kernel,
        grid=(bn, seq // rows),
        in_specs=[
            _row_spec(rows, d),
            _layer_spec(layer, (d, 2 * D_FF)),
            _layer_spec(layer, (CONV_WIDTH, 2 * D_FF)),
            _layer_spec(layer, (1, 2 * D_FF)),
            _layer_spec(layer, (D_FF, d)),
            _layer_spec(layer, (1, d)),
            _layer_spec(layer, (1, d)),
        ],
        out_specs=_row_spec(rows, d),
        out_shape=jax.ShapeDtypeStruct(x.shape, F32),
        scratch_shapes=[
            pltpu.VMEM((rows, D_FF), BF16),
            pltpu.VMEM((4 * n_chunks, SUBLANES, COL_CHUNK), F32),
        ],
        compiler_params=_params(),
        name="ffn",
    )(x, w_up, conv_w, conv_b, w_down, ln_g, ln_b)


def _odd_kernel(x_ref, win_ref, cw_ref, cb_ref, wout_ref, g_ref, b_ref, o_ref,
                y_ref, carry_ref):
    rows = x_ref.shape[1]
    d = x_ref.shape[2]
    n_chunks = d // COL_CHUNK

    @pl.when(pl.program_id(1) == 0)
    def _():
        carry_ref[...] = jnp.zeros_like(carry_ref)

    x = x_ref[0]
    xb = x.astype(BF16)
    for c in range(n_chunks):
        lo = c * COL_CHUNK
        hi = lo + COL_CHUNK
        b_gate = jnp.dot(xb, win_ref[:, lo:hi], preferred_element_type=F32)
        c_gate = jnp.dot(xb, win_ref[:, d + lo:d + hi], preferred_element_type=F32)
        hval = jnp.dot(xb, win_ref[:, 2 * d + lo:2 * d + hi], preferred_element_type=F32)
        conv = _causal_conv3(carry_ref, c, c_gate * hval, cw_ref[:, lo:hi], rows)
        y_ref[:, lo:hi] = (b_gate * (conv + cb_ref[:, lo:hi])).astype(BF16)
    m = jnp.dot(y_ref[...], wout_ref[...], preferred_element_type=F32)
    o_ref[0] = _layer_norm(DN_ALPHA * x + m, g_ref[...], b_ref[...])


def _odd_call(x, layer, mixer, w_in, conv_w, conv_b, w_out, ln_g, ln_b):
    bn, seq, d = x.shape
    rows = ROW_TILE
    return pl.pallas_call(
        _odd_kernel,
        grid=(bn, seq // rows),
        in_specs=[
            _row_spec(rows, d),
            _layer_spec(mixer, (d, 3 * d)),
            _layer_spec(mixer, (CONV_WIDTH, d)),
            _layer_spec(mixer, (1, d)),
            _layer_spec(mixer, (d, d)),
            _layer_spec(layer, (1, d)),
            _layer_spec(layer, (1, d)),
        ],
        out_specs=_row_spec(rows, d),
        out_shape=jax.ShapeDtypeStruct(x.shape, F32),
        scratch_shapes=[
            pltpu.VMEM((rows, d), BF16),
            pltpu.VMEM((2 * (d // COL_CHUNK), SUBLANES, COL_CHUNK), F32),
        ],
        compiler_params=_params(),
        name="odd_mixer",
    )(x, w_in, conv_w, conv_b, w_out, ln_g, ln_b)


def _even_in_kernel(x_ref, win_ref, wpool_ref, bpool_ref, pscale_ref,
                    q_ref, k_ref, vt_ref, kmean_ref, p_ref, ubuf_ref):
    rows = x_ref.shape[1]
    blocks = rows // MOBA_BLOCK
    s = pl.program_id(1)
    xb = x_ref[0].astype(BF16)

    q = jnp.dot(xb, win_ref[:, 0:ATT_WIDTH], preferred_element_type=F32)
    q_ref[0] = (q * (HEAD_DIM ** -0.5 * LOG2_E)).astype(BF16)

    k = jnp.dot(xb, win_ref[:, ATT_WIDTH:2 * ATT_WIDTH], preferred_element_type=F32)
    v = jnp.dot(xb, win_ref[:, 2 * ATT_WIDTH:3 * ATT_WIDTH], preferred_element_type=F32)
    for r in range(blocks):
        kb = k[r * MOBA_BLOCK:(r + 1) * MOBA_BLOCK, :]
        k_ref[0, r] = kb.astype(BF16)
        kmean_ref[0, pl.ds(s * blocks + r, 1), :] = (
            jnp.sum(kb, axis=0, keepdims=True) * (1.0 / MOBA_BLOCK))
        vt_ref[0, r] = v[r * MOBA_BLOCK:(r + 1) * MOBA_BLOCK, :].T.astype(BF16)

    u = jnp.dot(xb, win_ref[:, 3 * ATT_WIDTH:3 * ATT_WIDTH + POOL_WIDTH],
                preferred_element_type=F32)

    @pl.when(s == 0)
    def _():
        ubuf_ref[0:POOL_HALO, :] = jnp.zeros((POOL_HALO, POOL_WIDTH), F32)

    @pl.when(s != 0)
    def _():
        ubuf_ref[0:POOL_HALO, :] = ubuf_ref[rows:rows + POOL_HALO, :]

    ubuf_ref[POOL_HALO:POOL_HALO + rows, :] = u
    t = s * rows + lax.broadcasted_iota(jnp.int32, (rows, POOL_GROUP_DIM), 0)
    for g, win in enumerate(POOL_WINDOWS):
        lo = g * POOL_GROUP_DIM
        hi = lo + POOL_GROUP_DIM
        ug = u[:, lo:hi]
        wsum = ug
        for dlt in range(1, win):
            wsum = wsum + ubuf_ref[POOL_HALO - dlt:POOL_HALO - dlt + rows, lo:hi]
        count = jnp.minimum(t + 1, win).astype(F32)
        pooled = (wsum / count - ug).astype(BF16)
        mixed = jnp.dot(pooled, wpool_ref[g], preferred_element_type=F32) + bpool_ref[:, lo:hi]
        p_ref[0, :, lo:hi] = (mixed * pscale_ref[:, lo:hi]).astype(BF16)


def _even_in_call(x, mixer, w_in, w_pool, b_pool, pool_scale):
    bn, seq, d = x.shape
    rows = ROW_TILE
    nb = seq // MOBA_BLOCK
    blocks = rows // MOBA_BLOCK
    groups = len(POOL_WINDOWS)
    return pl.pallas_call(
        _even_in_kernel,
        grid=(bn, seq // rows),
        in_specs=[
            _row_spec(rows, d),
            _layer_spec(mixer, (d, 3 * ATT_WIDTH + POOL_WIDTH)),
            _layer_spec(mixer, (groups, POOL_GROUP_DIM, POOL_GROUP_DIM)),
            _layer_spec(mixer, (1, POOL_WIDTH)),
            _layer_spec(mixer, (1, POOL_WIDTH)),
        ],
        out_specs=[
            _row_spec(rows, ATT_WIDTH),
            pl.BlockSpec((1, blocks, MOBA_BLOCK, ATT_WIDTH), lambda b, s: (b, s, 0, 0)),
            pl.BlockSpec((1, blocks, ATT_WIDTH, MOBA_BLOCK), lambda b, s: (b, s, 0, 0)),
            pl.BlockSpec((1, nb, ATT_WIDTH), lambda b, s: (b, 0, 0)),
            _row_spec(rows, POOL_WIDTH),
        ],
        out_shape=[
            jax.ShapeDtypeStruct((bn, seq, ATT_WIDTH), BF16),
            jax.ShapeDtypeStruct((bn, nb, MOBA_BLOCK, ATT_WIDTH), BF16),
            jax.ShapeDtypeStruct((bn, nb, ATT_WIDTH, MOBA_BLOCK), BF16),
            jax.ShapeDtypeStruct((bn, nb, ATT_WIDTH), F32),
            jax.ShapeDtypeStruct((bn, seq, POOL_WIDTH), BF16),
        ],
        scratch_shapes=[pltpu.VMEM((POOL_HALO + rows, POOL_WIDTH), F32)],
        compiler_params=_params(),
        name="even_in",
    )(x, w_in, w_pool, b_pool, pool_scale)


def _attn_kernel(x_ref, q_ref, k_ref, vt_ref, kmean_ref, p_ref, wout_ref, g_ref, b_ref,
                 o_ref, qm_ref, sel_ref, m_ref, l_ref, acc_ref, s_ref, pr_ref, al_ref):
    blk = pl.program_id(1)
    nb = kmean_ref.shape[1]
    bs = MOBA_BLOCK
    pair = 2 * HEAD_DIM
    contract_last = (((1,), (1,)), ((), ()))

    m_ref[...] = jnp.full(m_ref.shape, NEG, F32)
    l_ref[...] = jnp.zeros(l_ref.shape, F32)
    acc_ref[...] = jnp.zeros(acc_ref.shape, F32)

    lane = lax.broadcasted_iota(jnp.int32, (bs, pair), 1)
    row_id = lax.broadcasted_iota(jnp.int32, (nb, bs), 0)
    for h in range(ATT_HEADS):
        j, e = divmod(h, 2)
        q_pair = q_ref[0, :, j * pair:(j + 1) * pair]
        qm = jnp.where((lane >= e * HEAD_DIM) & (lane < (e + 1) * HEAD_DIM), q_pair,
                       jnp.zeros_like(q_pair))
        qm_ref[h] = qm
        km = kmean_ref[0, :, j * pair:(j + 1) * pair].astype(BF16)
        gate = lax.dot_general(km, qm, contract_last, preferred_element_type=F32)
        gate = jnp.where(row_id < blk, gate, NEG)
        chosen = jnp.zeros((nb, bs), jnp.bool_)
        for i in range(MOBA_TOPK):
            top = jnp.max(gate, axis=0, keepdims=True)
            first = jnp.min(jnp.where(gate == top, row_id, nb), axis=0, keepdims=True)
            hit = row_id == first
            chosen = chosen | (hit & (i < blk))
            gate = jnp.where(hit, -jnp.inf, gate)
        sel_ref[h * nb:(h + 1) * nb, :] = chosen.astype(F32)

    ones_rows = jnp.ones((2 * SUBLANES, bs), BF16)


    def stage_scores(n, slot):
        for h in range(ATT_HEADS):
            j = h // 2
            s_ref[slot, h] = lax.dot_general(k_ref[0, n, :, j * pair:(j + 1) * pair], qm_ref[h],
                                             contract_last, preferred_element_type=F32)

    def stage_softmax(s_slot, p_slot, key_mask, query_sel):
        for h in range(ATT_HEADS):
            s_t = s_ref[s_slot, h]
            if key_mask is not None:
                s_t = jnp.where(key_mask, s_t, NEG)
            r = h * SUBLANES
            m_old = m_ref[r:r + 1, :]
            b_max = jnp.max(s_t, axis=0, keepdims=True)
            if query_sel is not None:
                sel = query_sel(h)
                b_max = jnp.where(sel, b_max, NEG)
            m_new = jnp.maximum(m_old, b_max)
            m_sub = m_new if query_sel is None else jnp.where(sel, m_new, -NEG)
            m_ref[r:r + 1, :] = m_new
            pr_ref[p_slot, h] = jnp.exp2(s_t - m_sub).astype(BF16)
            al_ref[p_slot, r:r + 1, :] = jnp.exp2(m_old - m_new)

    def stage_values(n, p_slot):
        for h in range(ATT_HEADS):
            lo = h * HEAD_DIM
            r = h * SUBLANES
            alpha = al_ref[p_slot, r:r + 1, :]
            lhs = jnp.concatenate([vt_ref[0, n, lo:lo + HEAD_DIM, :], ones_rows], axis=0)
            pv = jnp.dot(lhs, pr_ref[p_slot, h], preferred_element_type=F32)
            acc_ref[lo:lo + HEAD_DIM, :] = alpha * acc_ref[lo:lo + HEAD_DIM, :] + pv[0:HEAD_DIM, :]
            l_ref[r:r + 1, :] = alpha * l_ref[r:r + 1, :] + pv[HEAD_DIM:HEAD_DIM + 1, :]

    key_id = lax.broadcasted_iota(jnp.int32, (bs, bs), 0)
    qry_id = lax.broadcasted_iota(jnp.int32, (bs, bs), 1)
    stage_scores(blk, 0)
    stage_scores(0, 1)
    stage_softmax(0, 0, key_id <= qry_id, None)

    def step(t, cur, nxt):
        stage_scores(jnp.minimum(t + 1, nb - 1), cur)
        stage_values(jnp.where(t == 0, blk, t - 1), cur)
        live = t < blk
        row = jnp.minimum(t, nb - 1)
        stage_softmax(nxt, nxt, None,
                      lambda h: (sel_ref[pl.ds(h * nb + row, 1), :] > 0.5) & live)

    def two_steps(i, carry):
        step(2 * i, 0, 1)
        step(2 * i + 1, 1, 0)
        return carry

    lax.fori_loop(0, (blk + 2) // 2, two_steps, 0)

    for h in range(ATT_HEADS):
        r = h * SUBLANES
        lo = h * HEAD_DIM
        acc_ref[lo:lo + HEAD_DIM, :] = acc_ref[lo:lo + HEAD_DIM, :] * (1.0 / l_ref[r:r + 1, :])
    a = acc_ref[...].astype(BF16).T
    m = (jnp.dot(a, wout_ref[0:ATT_WIDTH, :], preferred_element_type=F32)
         + jnp.dot(p_ref[0], wout_ref[ATT_WIDTH:, :], preferred_element_type=F32))
    o_ref[0] = _layer_norm(DN_ALPHA * x_ref[0] + m, g_ref[...], b_ref[...])


def _attn_call(x, layer, mixer, q, k, vt, kmean, p, w_out, ln_g, ln_b):
    bn, seq, d = x.shape
    nb = seq // MOBA_BLOCK
    rows = MOBA_BLOCK
    return pl.pallas_call(
        _attn_kernel,
        grid=(bn, nb),
        in_specs=[
            _row_spec(rows, d),
            _row_spec(rows, ATT_WIDTH),
            pl.BlockSpec((1, nb, MOBA_BLOCK, ATT_WIDTH), lambda b, s: (b, 0, 0, 0)),
            pl.BlockSpec((1, nb, ATT_WIDTH, MOBA_BLOCK), lambda b, s: (b, 0, 0, 0)),
            pl.BlockSpec((1, nb, ATT_WIDTH), lambda b, s: (b, 0, 0)),
            _row_spec(rows, POOL_WIDTH),
            _layer_spec(mixer, (d, d)),
            _layer_spec(layer, (1, d)),
            _layer_spec(layer, (1, d)),
        ],
        out_specs=_row_spec(rows, d),
        out_shape=jax.ShapeDtypeStruct(x.shape, F32),
        scratch_shapes=[
            pltpu.VMEM((ATT_HEADS, MOBA_BLOCK, 2 * HEAD_DIM), BF16),
            pltpu.VMEM((ATT_HEADS * nb, MOBA_BLOCK), F32),
            pltpu.VMEM((ATT_HEADS * SUBLANES, MOBA_BLOCK), F32),
            pltpu.VMEM((ATT_HEADS * SUBLANES, MOBA_BLOCK), F32),
            pltpu.VMEM((ATT_WIDTH, MOBA_BLOCK), F32),
            pltpu.VMEM((2, ATT_HEADS, MOBA_BLOCK, MOBA_BLOCK), F32),
            pltpu.VMEM((2, ATT_HEADS, MOBA_BLOCK, MOBA_BLOCK), BF16),
            pltpu.VMEM((2, ATT_HEADS * SUBLANES, MOBA_BLOCK), F32),
        ],
        compiler_params=_params(),
        name="moba_attn",
    )(x, q, k, vt, kmean, p, w_out, ln_g, ln_b)


def kernel(x, even_w_in, even_w_pool, even_b_pool, even_pool_scale, even_w_out, odd_w_in, odd_conv_w, odd_conv_b, odd_w_out, mix_ln_g, mix_ln_b, ffn_w_up, ffn_conv_w, ffn_conv_b, ffn_w_down, ffn_ln_g, ffn_ln_b):
    groups = len(POOL_WINDOWS)
    rows = lambda a: a.reshape(a.shape[0], 1, a.shape[1])
    even_w_in, even_w_out = even_w_in.astype(BF16), even_w_out.astype(BF16)
    even_w_pool = even_w_pool.reshape(-1, groups, POOL_GROUP_DIM, POOL_GROUP_DIM).astype(BF16)
    odd_w_in, odd_w_out = odd_w_in.astype(BF16), odd_w_out.astype(BF16)
    ffn_w_up, ffn_w_down = ffn_w_up.astype(BF16), ffn_w_down.astype(BF16)
    even_b_pool, even_pool_scale, odd_conv_b = rows(even_b_pool), rows(even_pool_scale), rows(odd_conv_b)
    mix_ln_g, mix_ln_b, ffn_conv_b = rows(mix_ln_g), rows(mix_ln_b), rows(ffn_conv_b)
    ffn_ln_g, ffn_ln_b = rows(ffn_ln_g), rows(ffn_ln_b)
    for l in range(DEPTH):
        i = l // 2
        if l % 2 == 0:
            q, k, vt, kmean, p = _even_in_call(x, i, even_w_in, even_w_pool, even_b_pool, even_pool_scale)
            x = _attn_call(x, l, i, q, k, vt, kmean, p, even_w_out, mix_ln_g, mix_ln_b)
        else:
            x = _odd_call(x, l, i, odd_w_in, odd_conv_w, odd_conv_b, odd_w_out, mix_ln_g, mix_ln_b)
        x = _ffn_call(x, l, ffn_w_up, ffn_conv_w, ffn_conv_b, ffn_w_down, ffn_ln_g, ffn_ln_b)
    return x
```
